```python
import jax, jax.numpy as jnp
from jax import lax
import numpy as np

D_MODEL = 2048
BATCH = 4
SEQ = 8192
DEPTH = 2

MLA_HEADS = 8
MLA_NOPE = 128
MLA_ROPE = 64
MLA_V = 128
MLA_Q_LORA = D_MODEL // 4
MLA_KV_LORA = D_MODEL // 8
ROPE_THETA = 10000.0
Q_BLOCK = 128
DIL_HEADS = 8
DIL_HEAD_DIM = 128
DIL_WINDOWS = (128, 512, 2048)
DIL_RATES = (1, 4, 16)
DIL_WIDTH = DIL_HEADS * DIL_HEAD_DIM
MLA_WIDTH = MLA_HEADS * MLA_V
MIX_WIDTH = MLA_WIDTH + DIL_WIDTH
IN_COLS = MLA_Q_LORA + MLA_KV_LORA + MLA_ROPE + 3 * DIL_WIDTH
N_EXPERTS = 16
N_GROUPS = 4
TOP_K = 2
D_EXPERT = D_MODEL // 2
EXPERT_BLOCK = 256
N_ADA = 6
EPS = 1e-6
F32 = jnp.float32

kernel_name = "hybrid_mla_dilated_alibi_grouped_moe"


def rms_norm(x, g):
    xf = x.astype(F32)
    y = xf * lax.rsqrt(jnp.mean(xf * xf, axis=-1, keepdims=True) + EPS)
    return (y * g.astype(F32)).astype(x.dtype)


def rope(x, pos):
    r = x.shape[-1]
    half = r // 2
    inv = ROPE_THETA ** (-jnp.arange(half, dtype=F32) * 2.0 / r)
    ang = pos.astype(F32)[:, None] * inv[None, :]
    cos = jnp.cos(ang)[None, :, None, :]
    sin = jnp.sin(ang)[None, :, None, :]
    x1 = x[..., :half].astype(F32)
    x2 = x[..., half:].astype(F32)
    return jnp.concatenate([x1 * cos - x2 * sin, x2 * cos + x1 * sin], axis=-1).astype(x.dtype)


def alibi_slopes(n):
    return jnp.asarray([2.0 ** (-8.0 * (i + 1) / n) for i in range(n)], dtype=F32)


def causal_block_attention(q, k, v, scale):
    b, s, h, e = q.shape
    nq = s // Q_BLOCK
    qb = q.reshape(b, nq, Q_BLOCK, h, e).transpose(1, 0, 2, 3, 4)
    kpos = jnp.arange(s)

    def one_block(args):
        qblk, i = args
        sc = jnp.einsum('bqhe,bkhe->bhqk', qblk, k, preferred_element_type=F32) * scale
        qpos = i * Q_BLOCK + jnp.arange(Q_BLOCK)
        sc = jnp.where(kpos[None, :] <= qpos[:, None], sc, -jnp.inf)
        p = jax.nn.softmax(sc, axis=-1)
        return jnp.einsum('bhqk,bkhe->bqhe', p.astype(v.dtype), v)

    o = lax.map(one_block, (qb, jnp.arange(nq)))
    return o.transpose(1, 0, 2, 3, 4).reshape(b, s, h, v.shape[-1])


def dilated_branch(q, k, v, window, rate, slopes):
    b, s, h, e = q.shape
    band = window // rate
    n_sub = s // rate
    nb = -(-n_sub // band)
    n_pad = nb * band
    scale = e ** -0.5

    def by_stride(t):
        t = t.reshape(b, n_sub, rate, h, e)
        return jnp.pad(t, ((0, 0), (0, n_pad - n_sub), (0, 0), (0, 0), (0, 0)))

    def windows(t):
        t = jnp.pad(by_stride(t), ((0, 0), (band, 0), (0, 0), (0, 0), (0, 0)))
        t = t.reshape(b, nb + 1, band, rate, h, e)
        return jnp.concatenate([t[:, :-1], t[:, 1:]], axis=2)

    qs = by_stride(q).reshape(b, nb, band, rate, h, e)
    kw = windows(k)
    vw = windows(v)
    sc = jnp.einsum('bnqrhe,bnkrhe->bnrhqk', qs, kw, preferred_element_type=F32) * scale
    qi = jnp.arange(band)[:, None]
    kj = jnp.arange(2 * band)[None, :]
    delta = qi + band - kj
    blk = jnp.arange(nb)[:, None, None]
    valid = (delta >= 0) & (delta <= band) & (blk * band + kj - band >= 0)
    bias = -slopes[:, None, None] * (rate * delta).astype(F32)[None]
    sc = jnp.where(valid[None, :, None, None], sc + bias[None, None, None], -jnp.inf)
    lse = jax.nn.logsumexp(sc, axis=-1)
    p = jnp.exp(sc - lse[..., None])
    o = jnp.einsum('bnrhqk,bnkrhe->bnqrhe', p.astype(v.dtype), vw)
    o = o.reshape(b, n_pad, rate, h, e)[:, :n_sub].reshape(b, s, h, e)
    lse = lse.transpose(0, 1, 4, 2, 3).reshape(b, n_pad, rate, h)[:, :n_sub].reshape(b, s, h)
    return o, lse


def hybrid_mixer(h, w_in, q_norm_g, kv_norm_g, w_uq, w_ukv, out_norm_a_g, out_norm_b_g, w_out):
    b, s, _ = h.shape
    proj = h @ w_in
    o1 = MLA_Q_LORA
    o2 = o1 + MLA_KV_LORA
    o3 = o2 + MLA_ROPE
    o4 = o3 + DIL_WIDTH
    o5 = o4 + DIL_WIDTH
    c_q, c_kv, k_rope, q_b, k_b, v_b = jnp.split(proj, [o1, o2, o3, o4, o5], axis=-1)
    pos = jnp.arange(s)

    q_a = (rms_norm(c_q, q_norm_g) @ w_uq).reshape(b, s, MLA_HEADS, MLA_NOPE + MLA_ROPE)
    kv_a = (rms_norm(c_kv, kv_norm_g) @ w_ukv).reshape(b, s, MLA_HEADS, MLA_NOPE + MLA_V)
    q_a = jnp.concatenate([q_a[..., :MLA_NOPE], rope(q_a[..., MLA_NOPE:], pos)], axis=-1)
    k_r = rope(k_rope[:, :, None, :], pos)
    k_a = jnp.concatenate([kv_a[..., :MLA_NOPE], jnp.broadcast_to(k_r, (b, s, MLA_HEADS, MLA_ROPE))], axis=-1)
    v_a = kv_a[..., MLA_NOPE:]
    o_a = causal_block_attention(q_a, k_a, v_a, (MLA_NOPE + MLA_ROPE) ** -0.5).reshape(b, s, MLA_WIDTH)

    qd = q_b.reshape(b, s, DIL_HEADS, DIL_HEAD_DIM)
    kd = k_b.reshape(b, s, DIL_HEADS, DIL_HEAD_DIM)
    vd = v_b.reshape(b, s, DIL_HEADS, DIL_HEAD_DIM)
    slopes = alibi_slopes(DIL_HEADS)
    outs = []
    lses = []
    for window, rate in zip(DIL_WINDOWS, DIL_RATES):
        o_i, l_i = dilated_branch(qd, kd, vd, window, rate, slopes)
        outs.append(o_i)
        lses.append(l_i)
    alpha = jax.nn.softmax(jnp.stack(lses, axis=0), axis=0)
    o_b = jnp.einsum('nbsh,nbshe->bshe', alpha.astype(vd.dtype), jnp.stack(outs, axis=0)).reshape(b, s, DIL_WIDTH)

    y = jnp.concatenate([rms_norm(o_a, out_norm_a_g), rms_norm(o_b, out_norm_b_g)], axis=-1)
    return y @ w_out


def route(hf, router_w, router_b):
    t = hf.shape[0]
    scores = jax.nn.sigmoid((hf @ router_w).astype(F32))
    biased = scores + router_b.astype(F32)
    grouped = biased.reshape(t, N_GROUPS, N_EXPERTS // N_GROUPS)
    group_score = lax.top_k(grouped, 2)[0].sum(-1)
    g_sel = jnp.argmax(group_score, axis=-1)
    in_group = (jnp.arange(N_EXPERTS) // (N_EXPERTS // N_GROUPS))[None, :] == g_sel[:, None]
    _, idx = lax.top_k(jnp.where(in_group, biased, -jnp.inf), TOP_K)
    w = jnp.take_along_axis(scores, idx, axis=-1)
    w = w / jnp.sum(w, axis=-1, keepdims=True)
    return idx, w


def moe(h, router_w, router_b, w_gate, w_up, w_down):
    b, s, d = h.shape
    t = b * s
    hf = h.reshape(t, d)
    idx, wts = route(hf, router_w, router_b)
    n_assign = t * TOP_K
    flat_e = idx.reshape(-1)
    flat_tok = jnp.arange(n_assign, dtype=jnp.int32) // TOP_K
    flat_w = wts.reshape(-1)
    order = jnp.argsort(flat_e)
    e_sorted = flat_e[order]
    counts = jnp.bincount(flat_e, length=N_EXPERTS)
    padded = (counts + EXPERT_BLOCK - 1) // EXPERT_BLOCK * EXPERT_BLOCK
    pad_end = jnp.cumsum(padded)
    pad_start = pad_end - padded
    start = jnp.cumsum(counts) - counts
    dest = pad_start[e_sorted] + (jnp.arange(n_assign) - start[e_sorted])
    n_rows = n_assign + N_EXPERTS * EXPERT_BLOCK
    n_blocks = n_rows // EXPERT_BLOCK
    row_tok = jnp.full((n_rows,), t, dtype=jnp.int32).at[dest].set(flat_tok[order])
    row_w = jnp.zeros((n_rows,), h.dtype).at[dest].set(flat_w[order].astype(h.dtype))
    blk_e = jnp.minimum(jnp.searchsorted(pad_end, jnp.arange(n_blocks) * EXPERT_BLOCK, side='right'), N_EXPERTS - 1)
    h_pad = jnp.concatenate([hf, jnp.zeros((1, d), hf.dtype)], axis=0)

    def expert_block(args):
        tok, wt, e = args
        xb = h_pad[tok]
        a = xb @ w_gate[e]
        u = xb @ w_up[e]
        return ((jax.nn.silu(a) * u) @ w_down[e]) * wt[:, None]

    y = lax.map(expert_block, (row_tok.reshape(n_blocks, EXPERT_BLOCK), row_w.reshape(n_blocks, EXPERT_BLOCK), blk_e))
    out = jnp.zeros((t + 1, d), h.dtype).at[row_tok].add(y.reshape(n_rows, d))[:t]
    return out.reshape(b, s, d)


def setup_inputs(seed: int = 0) -> dict:
    key = jax.random.key(seed)
    ks = jax.random.split(key, 20)
    nrm = jax.random.normal
    L = DEPTH
    return {
        "x": nrm(ks[0], (BATCH, SEQ, D_MODEL), F32),
        "c": nrm(ks[1], (BATCH, D_MODEL), F32),
        "ada_w": nrm(ks[2], (L, D_MODEL, N_ADA * D_MODEL), F32) * (0.5 * D_MODEL ** -0.5),
        "ada_b": nrm(ks[3], (L, N_ADA * D_MODEL), F32) * 0.02,
        "norm1_g": 1.0 + 0.05 * nrm(ks[4], (L, D_MODEL), F32),
        "norm2_g": 1.0 + 0.05 * nrm(ks[5], (L, D_MODEL), F32),
        "w_in": nrm(ks[6], (L, D_MODEL, IN_COLS), F32) * D_MODEL ** -0.5,
        "q_norm_g": 1.0 + 0.05 * nrm(ks[7], (L, MLA_Q_LORA), F32),
        "kv_norm_g": 1.0 + 0.05 * nrm(ks[8], (L, MLA_KV_LORA), F32),
        "w_uq": nrm(ks[9], (L, MLA_Q_LORA, MLA_HEADS * (MLA_NOPE + MLA_ROPE)), F32) * MLA_Q_LORA ** -0.5,
        "w_ukv": nrm(ks[10], (L, MLA_KV_LORA, MLA_HEADS * (MLA_NOPE + MLA_V)), F32) * MLA_KV_LORA ** -0.5,
        "out_norm_a_g": 1.0 + 0.05 * nrm(ks[11], (L, MLA_WIDTH), F32),
        "out_norm_b_g": 1.0 + 0.05 * nrm(ks[12], (L, DIL_WIDTH), F32),
        "w_out": nrm(ks[13], (L, MIX_WIDTH, D_MODEL), F32) * MIX_WIDTH ** -0.5,
        "router_w": nrm(ks[14], (D_MODEL, N_EXPERTS), F32) * D_MODEL ** -0.5,
        "router_b": nrm(ks[15], (N_EXPERTS,), F32) * 0.01,
        "exp_gate": nrm(ks[16], (L, N_EXPERTS, D_MODEL, D_EXPERT), F32) * D_MODEL ** -0.5,
        "exp_up": nrm(ks[17], (L, N_EXPERTS, D_MODEL, D_EXPERT), F32) * D_MODEL ** -0.5,
        "exp_down": nrm(ks[18], (L, N_EXPERTS, D_EXPERT, D_MODEL), F32) * D_EXPERT ** -0.5,
        "final_g": 1.0 + 0.05 * nrm(ks[19], (D_MODEL,), F32),
    }


def reference(x, c, ada_w, ada_b, norm1_g, norm2_g, w_in, q_norm_g, kv_norm_g, w_uq, w_ukv,
              out_norm_a_g, out_norm_b_g, w_out, router_w, router_b, exp_gate, exp_up, exp_down, final_g):
    c_act = jax.nn.silu(c)
    for l in range(DEPTH):
        mod = (c_act @ ada_w[l] + ada_b[l])[:, None, :]
        sh1, sc1, g1, sh2, sc2, g2 = jnp.split(mod, N_ADA, axis=-1)
        h = rms_norm(x, norm1_g[l]) * (1.0 + sc1) + sh1
        x = x + g1 * hybrid_mixer(h, w_in[l], q_norm_g[l], kv_norm_g[l], w_uq[l], w_ukv[l],
                                  out_norm_a_g[l], out_norm_b_g[l], w_out[l])
        h = rms_norm(x, norm2_g[l]) * (1.0 + sc2) + sh2
        x = x + g2 * moe(h, router_w, router_b, exp_gate[l], exp_up[l], exp_down[l])
    return rms_norm(x, final_g)
```

```python
import functools

import jax
import jax.numpy as jnp
import numpy as np
from jax import lax
from jax.experimental import pallas as pl
from jax.experimental.pallas import tpu as pltpu

F32 = jnp.float32
BF16 = jnp.bfloat16

D_MODEL = 2048
DEPTH = 2
MLA_HEADS = 8
MLA_NOPE = 128
MLA_ROPE = 64
MLA_V = 128
MLA_Q_LORA = D_MODEL // 4
MLA_KV_LORA = D_MODEL // 8
ROPE_THETA = 10000.0
DIL_HEADS = 8
DIL_HEAD_DIM = 128
DIL_WINDOWS = (128, 512, 2048)
DIL_RATES = (1, 4, 16)
DIL_WIDTH = DIL_HEADS * DIL_HEAD_DIM
MLA_WIDTH = MLA_HEADS * MLA_V
N_EXPERTS = 16
N_GROUPS = 4
GROUP_SIZE = N_EXPERTS // N_GROUPS
D_EXPERT = D_MODEL // 2
N_ADA = 6
EPS = 1e-6

LANES = 128
MLA_QK_PAD = 2 * LANES
PROJ_COLS = 4096
COL_Q_B, COL_K_B, COL_V_B = 0, DIL_WIDTH, 2 * DIL_WIDTH
COL_CQ = 3 * DIL_WIDTH
COL_CKV = COL_CQ + MLA_Q_LORA
COL_KR = COL_CKV + MLA_KV_LORA
BAND = 128
MOE_BLOCK = 512
VMEM_LIMIT = 56 * 1024 * 1024


def _cparams(sem):
    return pltpu.CompilerParams(dimension_semantics=sem, vmem_limit_bytes=VMEM_LIMIT)


def _ada_kernel(c_ref, w_ref, b_ref, o_ref):
    c = c_ref[...]
    act = (c * jax.nn.sigmoid(c)).astype(BF16)
    o_ref[0] = jnp.dot(act, w_ref[0].astype(BF16), preferred_element_type=F32) + b_ref[0]


def ada_mod(c_pad, ada_w, ada_b):
    depth, d, n = ada_w.shape
    rows = c_pad.shape[0]
    bn = 1024
    return pl.pallas_call(
        _ada_kernel,
        out_shape=jax.ShapeDtypeStruct((depth, rows, n), F32),
        grid=(depth, n // bn),
        in_specs=[
            pl.BlockSpec((rows, d), lambda l, j: (0, 0)),
            pl.BlockSpec((1, d, bn), lambda l, j: (l, 0, j)),
            pl.BlockSpec((1, 1, bn), lambda l, j: (l, 0, j)),
        ],
        out_specs=pl.BlockSpec((1, rows, bn), lambda l, j: (l, 0, j)),
        compiler_params=_cparams(("parallel", "parallel")),
        name="ada_mod",
    )(c_pad, ada_w, ada_b.reshape(depth, 1, n))


def _modulated_norm(x, g, sc, sh):
    ms = jnp.mean(x * x, axis=-1, keepdims=True)
    return (x * lax.rsqrt(ms + EPS) * g) * (1.0 + sc) + sh


def _norm_matmul_kernel(x_ref, g_ref, sc_ref, sh_ref, w_ref, o_ref, h_scr):
    @pl.when(pl.program_id(1) == 0)
    def _():
        h_scr[...] = _modulated_norm(x_ref[...], g_ref[...], sc_ref[0], sh_ref[0]).astype(BF16)

    o_ref[...] = jnp.dot(h_scr[...], w_ref[...], preferred_element_type=F32).astype(o_ref.dtype)


def norm_matmul(x, g, sc, sh, w, seq):
    t, d = x.shape
    n = w.shape[1]
    bm = min(1024, seq)
    bn = 1024
    per_b = seq // bm
    return pl.pallas_call(
        _norm_matmul_kernel,
        out_shape=jax.ShapeDtypeStruct((t, n), BF16),
        grid=(t // bm, n // bn),
        in_specs=[
            pl.BlockSpec((bm, d), lambda i, j: (i, 0)),
            pl.BlockSpec((1, d), lambda i, j: (0, 0)),
            pl.BlockSpec((1, 1, d), lambda i, j: (i // per_b, 0, 0)),
            pl.BlockSpec((1, 1, d), lambda i, j: (i // per_b, 0, 0)),
            pl.BlockSpec((d, bn), lambda i, j: (0, j)),
        ],
        out_specs=pl.BlockSpec((bm, bn), lambda i, j: (i, j)),
        scratch_shapes=[pltpu.VMEM((bm, d), BF16)],
        compiler_params=_cparams(("parallel", "arbitrary")),
        name="norm_matmul",
    )(x, g, sc, sh, w)


def _rms(x, g):
    ms = jnp.mean(x * x, axis=-1, keepdims=True)
    return x * lax.rsqrt(ms + EPS) * g


def _rope_lanes(x, cos, sin_up, sin_dn):
    return x * cos + pltpu.roll(x, 32, 1) * sin_up + pltpu.roll(x, 96, 1) * sin_dn


def _mla_proj_kernel(cq_ref, ckv_ref, kr_ref, qg_ref, kvg_ref, wuq_ref, wukv_ref,
                     cos_ref, sup_ref, sdn_ref, q_ref, k_ref, v_ref):
    cos, sup, sdn = cos_ref[...], sup_ref[...], sdn_ref[...]
    scale = (MLA_NOPE + MLA_ROPE) ** -0.5

    cqn = _rms(cq_ref[...].astype(F32), qg_ref[...]).astype(BF16)
    q = jnp.dot(cqn, wuq_ref[...], preferred_element_type=F32) * scale
    for h in range(MLA_HEADS):
        lo = h * MLA_QK_PAD
        q_ref[:, lo:lo + LANES] = q[:, lo:lo + LANES].astype(BF16)
        q_ref[:, lo + LANES:lo + 2 * LANES] = _rope_lanes(q[:, lo + LANES:lo + 2 * LANES], cos, sup, sdn).astype(BF16)

    ckvn = _rms(ckv_ref[...].astype(F32), kvg_ref[...]).astype(BF16)
    kv = jnp.dot(ckvn, wukv_ref[...], preferred_element_type=F32)
    krr = _rope_lanes(kr_ref[...].astype(F32), cos, sup, sdn).astype(BF16)
    for h in range(MLA_HEADS):
        lo = h * MLA_QK_PAD
        k_ref[:, lo:lo + LANES] = kv[:, lo:lo + LANES].astype(BF16)
        k_ref[:, lo + LANES:lo + 2 * LANES] = krr
        v_ref[:, h * MLA_V:(h + 1) * MLA_V] = kv[:, lo + LANES:lo + 2 * LANES].astype(BF16)


def mla_proj(proj, qg, kvg, wuq, wukv, cos, sup, sdn, seq):
    t = proj.shape[0]
    bm = 512
    per_b = seq // bm
    width = MLA_HEADS * MLA_QK_PAD
    return pl.pallas_call(
        _mla_proj_kernel,
        out_shape=(jax.ShapeDtypeStruct((t, width), BF16),
                   jax.ShapeDtypeStruct((t, width), BF16),
                   jax.ShapeDtypeStruct((t, MLA_WIDTH), BF16)),
        grid=(t // bm,),
        in_specs=[
            pl.BlockSpec((bm, MLA_Q_LORA), lambda i: (i, COL_CQ // MLA_Q_LORA)),
            pl.BlockSpec((bm, MLA_KV_LORA), lambda i: (i, COL_CKV // MLA_KV_LORA)),
            pl.BlockSpec((bm, LANES), lambda i: (i, COL_KR // LANES)),
            pl.BlockSpec((1, MLA_Q_LORA), lambda i: (0, 0)),
            pl.BlockSpec((1, MLA_KV_LORA), lambda i: (0, 0)),
            pl.BlockSpec((MLA_Q_LORA, width), lambda i: (0, 0)),
            pl.BlockSpec((MLA_KV_LORA, width), lambda i: (0, 0)),
            pl.BlockSpec((bm, LANES), lambda i: (i % per_b, 0)),
            pl.BlockSpec((bm, LANES), lambda i: (i % per_b, 0)),
            pl.BlockSpec((bm, LANES), lambda i: (i % per_b, 0)),
        ],
        out_specs=(pl.BlockSpec((bm, width), lambda i: (i, 0)),
                   pl.BlockSpec((bm, width), lambda i: (i, 0)),
                   pl.BlockSpec((bm, MLA_WIDTH), lambda i: (i, 0))),
        compiler_params=_cparams(("parallel",)),
        name="mla_proj",
    )(proj, proj, proj, qg, kvg, wuq, wukv, cos, sup, sdn)


def _mla_attn_kernel(q_ref, k_ref, v_ref, o_ref, *, blk):
    i = pl.program_id(2)
    q = q_ref[...]

    def step(j, carry, masked):
        m, l, acc = carry
        start = pl.multiple_of(j * blk, blk)
        k = k_ref[pl.ds(start, blk), :]
        v = v_ref[pl.ds(start, blk), :]
        s = lax.dot_general(q, k, (((1,), (1,)), ((), ())), preferred_element_type=F32)
        if masked:
            row = lax.broadcasted_iota(jnp.int32, s.shape, 0)
            col = lax.broadcasted_iota(jnp.int32, s.shape, 1)
            s = jnp.where(col <= row, s, -jnp.inf)
        m_new = jnp.maximum(m, jnp.max(s, axis=-1, keepdims=True))
        alpha = jnp.exp(m - m_new)
        p = jnp.exp(s - m_new)
        l = alpha * l + jnp.sum(p, axis=-1, keepdims=True)
        acc = alpha * acc + jnp.dot(p.astype(BF16), v, preferred_element_type=F32)
        return m_new, l, acc

    init = (jnp.full((blk, 1), -jnp.inf, F32), jnp.zeros((blk, 1), F32), jnp.zeros((blk, MLA_V), F32))
    carry = lax.fori_loop(0, i, lambda j, c: step(j, c, False), init)
    _, l, acc = step(i, carry, True)
    o_ref[...] = (acc / l).astype(o_ref.dtype)


def mla_attn(q, k, v, batch, seq):
    t = q.shape[0]
    blk = 512
    nq = seq // blk
    return pl.pallas_call(
        functools.partial(_mla_attn_kernel, blk=blk),
        out_shape=jax.ShapeDtypeStruct((t, MLA_WIDTH), BF16),
        grid=(batch, MLA_HEADS, nq),
        in_specs=[
            pl.BlockSpec((blk, MLA_QK_PAD), lambda b, h, i: (b * nq + i, h)),
            pl.BlockSpec((seq, MLA_QK_PAD), lambda b, h, i: (b, h)),
            pl.BlockSpec((seq, MLA_V), lambda b, h, i: (b, h)),
        ],
        out_specs=pl.BlockSpec((blk, MLA_V), lambda b, h, i: (b * nq + i, h)),
        compiler_params=_cparams(("parallel", "parallel", "arbitrary")),
        name="mla_attn",
    )(q, k, v)


def _alibi_slope(h):
    return 2.0 ** (-8.0 * (h + 1) / DIL_HEADS)


def _dil_attn_kernel(q_ref, k_ref, v_ref, kp_ref, vp_ref, o_ref, lse_ref, kc_scr, vc_scr, *, rate, rows):
    n = pl.program_id(2)
    kc_scr[0:BAND, :] = kp_ref[0]
    kc_scr[BAND:BAND + rows, :] = k_ref[0]
    vc_scr[0:BAND, :] = vp_ref[0]
    vc_scr[BAND:BAND + rows, :] = v_ref[0]

    qi = lax.broadcasted_iota(jnp.int32, (BAND, 2 * BAND), 0)
    kj = lax.broadcasted_iota(jnp.int32, (BAND, 2 * BAND), 1)
    delta = qi + BAND - kj
    in_band = (delta >= 0) & (delta <= BAND)
    dist = (rate * delta).astype(F32)
    scale = DIL_HEAD_DIM ** -0.5
    sub_per_step = rows // BAND

    def sub_block(t, _):
        r0 = pl.multiple_of(t * BAND, BAND)
        first = (n * sub_per_step + t) == 0
        valid = in_band & jnp.logical_or(jnp.logical_not(first), kj >= BAND)
        for h in range(DIL_HEADS):
            c0 = h * DIL_HEAD_DIM
            q = q_ref[0, pl.ds(r0, BAND), c0:c0 + DIL_HEAD_DIM]
            k = kc_scr[pl.ds(r0, 2 * BAND), c0:c0 + DIL_HEAD_DIM]
            v = vc_scr[pl.ds(r0, 2 * BAND), c0:c0 + DIL_HEAD_DIM]
            s = lax.dot_general(q, k, (((1,), (1,)), ((), ())), preferred_element_type=F32)
            s = jnp.where(valid, s * scale - _alibi_slope(h) * dist, -jnp.inf)
            m = jnp.max(s, axis=-1, keepdims=True)
            p = jnp.exp(s - m)
            l = jnp.sum(p, axis=-1, keepdims=True)
            o = jnp.dot(p.astype(BF16), v, preferred_element_type=F32) / l
            o_ref[0, pl.ds(r0, BAND), c0:c0 + DIL_HEAD_DIM] = o.astype(o_ref.dtype)
            lse_ref[0, 0, pl.ds(r0, BAND), h:h + 1] = m + jnp.log(l)
        return 0

    lax.fori_loop(0, sub_per_step, sub_block, 0)


def dil_attn(proj, batch, seq, rate):
    n_sub = seq // rate
    rows = min(n_sub, 1024)
    n_blk = n_sub // rows
    tiles = PROJ_COLS // DIL_WIDTH
    pv = proj.reshape(batch, n_sub, rate * PROJ_COLS)
    prev = lambda b, r, n: jnp.maximum(n * (rows // BAND) - 1, 0)
    o, lse = pl.pallas_call(
        functools.partial(_dil_attn_kernel, rate=rate, rows=rows),
        out_shape=(jax.ShapeDtypeStruct((batch, n_sub, rate * DIL_WIDTH), BF16),
                   jax.ShapeDtypeStruct((batch, rate, n_sub, DIL_HEADS), F32)),
        grid=(batch, rate, n_blk),
        in_specs=[
            pl.BlockSpec((1, rows, DIL_WIDTH), lambda b, r, n: (b, n, r * tiles + COL_Q_B // DIL_WIDTH)),
            pl.BlockSpec((1, rows, DIL_WIDTH), lambda b, r, n: (b, n, r * tiles + COL_K_B // DIL_WIDTH)),
            pl.BlockSpec((1, rows, DIL_WIDTH), lambda b, r, n: (b, n, r * tiles + COL_V_B // DIL_WIDTH)),
            pl.BlockSpec((1, BAND, DIL_WIDTH), lambda b, r, n: (b, prev(b, r, n), r * tiles + COL_K_B // DIL_WIDTH)),
            pl.BlockSpec((1, BAND, DIL_WIDTH), lambda b, r, n: (b, prev(b, r, n), r * tiles + COL_V_B // DIL_WIDTH)),
        ],
        out_specs=(pl.BlockSpec((1, rows, DIL_WIDTH), lambda b, r, n: (b, n, r)),
                   pl.BlockSpec((1, 1, rows, DIL_HEADS), lambda b, r, n: (b, r, n, 0))),
        scratch_shapes=[pltpu.VMEM((rows + BAND, DIL_WIDTH), BF16), pltpu.VMEM((rows + BAND, DIL_WIDTH), BF16)],
        compiler_params=_cparams(("parallel", "parallel", "arbitrary")),
        name=f"dil_attn_r{rate}",
    )(pv, pv, pv, pv, pv)
    o = o.reshape(batch * seq, DIL_WIDTH)
    lse = lse.transpose(0, 2, 1, 3).reshape(batch * seq, DIL_HEADS)
    return o, lse


def _out_proj_kernel(oa_ref, o1_ref, o2_ref, o3_ref, l1_ref, l2_ref, l3_ref, ga_ref, gb_ref,
                     w_ref, x_ref, g1_ref, xo_ref, y_scr):
    @pl.when(pl.program_id(1) == 0)
    def _():
        l1, l2, l3 = l1_ref[...], l2_ref[...], l3_ref[...]
        mx = jnp.maximum(jnp.maximum(l1, l2), l3)
        e1, e2, e3 = jnp.exp(l1 - mx), jnp.exp(l2 - mx), jnp.exp(l3 - mx)
        inv = 1.0 / (e1 + e2 + e3)
        a1, a2, a3 = e1 * inv, e2 * inv, e3 * inv
        parts = []
        for h in range(DIL_HEADS):
            c = slice(h * DIL_HEAD_DIM, (h + 1) * DIL_HEAD_DIM)
            parts.append(a1[:, h:h + 1] * o1_ref[:, c].astype(F32)
                         + a2[:, h:h + 1] * o2_ref[:, c].astype(F32)
                         + a3[:, h:h + 1] * o3_ref[:, c].astype(F32))
        ob = jnp.concatenate(parts, axis=-1)
        y_scr[:, 0:MLA_WIDTH] = _rms(oa_ref[...].astype(F32), ga_ref[...]).astype(BF16)
        y_scr[:, MLA_WIDTH:MLA_WIDTH + DIL_WIDTH] = _rms(ob, gb_ref[...]).astype(BF16)

    y = jnp.dot(y_scr[...], w_ref[...], preferred_element_type=F32)
    xo_ref[...] = x_ref[...] + g1_ref[0] * y


def out_proj(oa, obs, lses, ga, gb, w, x, g1, seq):
    t, d = x.shape
    bm = 512
    bn = 1024
    per_b = seq // bm
    row = lambda width: pl.BlockSpec((bm, width), lambda i, j: (i, 0))
    return pl.pallas_call(
        _out_proj_kernel,
        out_shape=jax.ShapeDtypeStruct((t, d), F32),
        grid=(t // bm, d // bn),
        in_specs=[
            row(MLA_WIDTH), row(DIL_WIDTH), row(DIL_WIDTH), row(DIL_WIDTH),
            row(DIL_HEADS), row(DIL_HEADS), row(DIL_HEADS),
            pl.BlockSpec((1, MLA_WIDTH), lambda i, j: (0, 0)),
            pl.BlockSpec((1, DIL_WIDTH), lambda i, j: (0, 0)),
            pl.BlockSpec((MLA_WIDTH + DIL_WIDTH, bn), lambda i, j: (0, j)),
            pl.BlockSpec((bm, bn), lambda i, j: (i, j)),
            pl.BlockSpec((1, 1, bn), lambda i, j: (i // per_b, 0, j)),
        ],
        out_specs=pl.BlockSpec((bm, bn), lambda i, j: (i, j)),
        scratch_shapes=[pltpu.VMEM((bm, MLA_WIDTH + DIL_WIDTH), BF16)],
        compiler_params=_cparams(("parallel", "arbitrary")),
        name="out_proj",
    )(oa, *obs, *lses, ga, gb, w, x, g1)


def _top2_sum(a, b, c, d):
    hi1, lo1 = jnp.maximum(a, b), jnp.minimum(a, b)
    hi2, lo2 = jnp.maximum(c, d), jnp.minimum(c, d)
    return jnp.maximum(hi1, hi2) + jnp.maximum(jnp.minimum(hi1, hi2), jnp.maximum(lo1, lo2))


def _router_kernel(x_ref, g_ref, sc_ref, sh_ref, rw_hi_ref, rw_lo_ref, rb_ref, tri_ref,
                   h_ref, info_ref, cnt_ref, base_scr):
    @pl.when(pl.program_id(0) == 0)
    def _():
        base_scr[...] = jnp.zeros_like(base_scr)

    h = _modulated_norm(x_ref[...], g_ref[...], sc_ref[0], sh_ref[0])
    h_hi = h.astype(BF16)
    h_ref[...] = h_hi
    h_lo = (h - h_hi.astype(F32)).astype(BF16)
    rw_hi = rw_hi_ref[...]
    logits = (jnp.dot(h_hi, rw_hi, preferred_element_type=F32)
              + jnp.dot(h_lo, rw_hi, preferred_element_type=F32)
              + jnp.dot(h_hi, rw_lo_ref[...], preferred_element_type=F32))
    lt = logits.T[0:N_EXPERTS, :]
    score = jax.nn.sigmoid(lt)
    biased = score + rb_ref[...]
    srow = [score[e:e + 1, :] for e in range(N_EXPERTS)]
    brow = [biased[e:e + 1, :] for e in range(N_EXPERTS)]

    gsum = [_top2_sum(*brow[GROUP_SIZE * g:GROUP_SIZE * (g + 1)]) for g in range(N_GROUPS)]
    best, gsel = gsum[0], jnp.zeros_like(gsum[0], dtype=jnp.int32)
    for g in range(1, N_GROUPS):
        better = gsum[g] > best
        gsel = jnp.where(better, g, gsel)
        best = jnp.where(better, gsum[g], best)

    def pick(rows, j):
        out = rows[j]
        for g in range(1, N_GROUPS):
            out = jnp.where(gsel == g, rows[GROUP_SIZE * g + j], out)
        return out

    bv = [pick(brow, j) for j in range(GROUP_SIZE)]
    sv = [pick(srow, j) for j in range(GROUP_SIZE)]
    v1, i1, s1 = bv[0], jnp.zeros_like(gsel), sv[0]
    for j in range(1, GROUP_SIZE):
        better = bv[j] > v1
        i1 = jnp.where(better, j, i1)
        s1 = jnp.where(better, sv[j], s1)
        v1 = jnp.where(better, bv[j], v1)
    v2 = jnp.full_like(v1, -jnp.inf)
    i2, s2 = jnp.zeros_like(gsel), jnp.zeros_like(s1)
    for j in range(GROUP_SIZE):
        better = (i1 != j) & (bv[j] > v2)
        i2 = jnp.where(better, j, i2)
        s2 = jnp.where(better, sv[j], s2)
        v2 = jnp.where(better, bv[j], v2)
    e1 = gsel * GROUP_SIZE + i1
    e2 = gsel * GROUP_SIZE + i2
    wsum = s1 + s2

    eid = lax.broadcasted_iota(jnp.int32, (N_EXPERTS, e1.shape[1]), 0)
    hit1 = eid == e1
    hit2 = eid == e2
    onehot = jnp.where(hit1 | hit2, 1.0, 0.0)
    prefix = jnp.dot(onehot.astype(BF16), tri_ref[...], preferred_element_type=F32) + base_scr[:, 0:1]
    r1 = jnp.sum(jnp.where(hit1, prefix, 0.0), axis=0, keepdims=True)
    r2 = jnp.sum(jnp.where(hit2, prefix, 0.0), axis=0, keepdims=True)
    new_base = base_scr[...] + jnp.sum(onehot, axis=1, keepdims=True)
    base_scr[...] = new_base
    cnt_ref[...] = new_base

    rows = (e1.astype(F32), e2.astype(F32), r1, r2, s1 / wsum, s2 / wsum, jnp.zeros_like(s1), jnp.zeros_like(s1))
    for r, val in enumerate(rows):
        info_ref[r:r + 1, :] = val


def router(x, g, sc, sh, rw_hi, rw_lo, rb, tri, seq):
    t, d = x.shape
    bm = tri.shape[0]
    per_b = seq // bm
    return pl.pallas_call(
        _router_kernel,
        out_shape=(jax.ShapeDtypeStruct((t, d), BF16),
                   jax.ShapeDtypeStruct((8, t), F32),
                   jax.ShapeDtypeStruct((N_EXPERTS, LANES), F32)),
        grid=(t // bm,),
        in_specs=[
            pl.BlockSpec((bm, d), lambda i: (i, 0)),
            pl.BlockSpec((1, d), lambda i: (0, 0)),
            pl.BlockSpec((1, 1, d), lambda i: (i // per_b, 0, 0)),
            pl.BlockSpec((1, 1, d), lambda i: (i // per_b, 0, 0)),
            pl.BlockSpec((d, LANES), lambda i: (0, 0)),
            pl.BlockSpec((d, LANES), lambda i: (0, 0)),
            pl.BlockSpec((N_EXPERTS, 1), lambda i: (0, 0)),
            pl.BlockSpec((bm, bm), lambda i: (0, 0)),
        ],
        out_specs=(pl.BlockSpec((bm, d), lambda i: (i, 0)),
                   pl.BlockSpec((8, bm), lambda i: (0, i)),
                   pl.BlockSpec((N_EXPERTS, LANES), lambda i: (0, 0))),
        scratch_shapes=[pltpu.VMEM((N_EXPERTS, LANES), F32)],
        compiler_params=_cparams(("arbitrary",)),
        name="router",
    )(x, g, sc, sh, rw_hi, rw_lo, rb, tri)


def _moe_ffn_kernel(blk_e_ref, n_used_ref, x_ref, wgu_ref, wd_ref, o_ref):
    used = pl.program_id(0) < n_used_ref[0]

    @pl.when(jnp.logical_not(used))
    def _():
        o_ref[...] = jnp.zeros_like(o_ref)

    @pl.when(used)
    def _():
        gu = jnp.dot(x_ref[...], wgu_ref[0], preferred_element_type=F32)
        a, u = gu[:, 0:D_EXPERT], gu[:, D_EXPERT:2 * D_EXPERT]
        mid = (a * jax.nn.sigmoid(a) * u).astype(BF16)
        o_ref[...] = jnp.dot(mid, wd_ref[0], preferred_element_type=F32).astype(o_ref.dtype)


def moe_ffn(blk_e, n_used, xs, wgu, wd):
    n_rows, d = xs.shape
    n_blocks = n_rows // MOE_BLOCK
    return pl.pallas_call(
        _moe_ffn_kernel,
        out_shape=jax.ShapeDtypeStruct((n_rows, d), BF16),
        grid_spec=pltpu.PrefetchScalarGridSpec(
            num_scalar_prefetch=2,
            grid=(n_blocks,),
            in_specs=[
                pl.BlockSpec((MOE_BLOCK, d), lambda i, be, nu: (i, 0)),
                pl.BlockSpec((1, d, 2 * D_EXPERT), lambda i, be, nu: (be[i], 0, 0)),
                pl.BlockSpec((1, D_EXPERT, d), lambda i, be, nu: (be[i], 0, 0)),
            ],
            out_specs=pl.BlockSpec((MOE_BLOCK, d), lambda i, be, nu: (i, 0)),
        ),
        compiler_params=_cparams(("arbitrary",)),
        name="moe_ffn",
    )(blk_e, n_used, xs, wgu, wd)


def _combine_kernel(x_ref, y1_ref, y2_ref, w_ref, g2_ref, fg_ref, o_ref, *, final):
    w = w_ref[...]
    moe = y1_ref[...].astype(F32) * w[:, 0:1] + y2_ref[...].astype(F32) * w[:, 1:2]
    x = x_ref[...] + g2_ref[0] * moe
    o_ref[...] = _rms(x, fg_ref[...]) if final else x


def combine(x, y1, y2, w12, g2, fg, seq, final):
    t, d = x.shape
    bm = 512
    per_b = seq // bm
    row = pl.BlockSpec((bm, d), lambda i: (i, 0))
    return pl.pallas_call(
        functools.partial(_combine_kernel, final=final),
        out_shape=jax.ShapeDtypeStruct((t, d), F32),
        grid=(t // bm,),
        in_specs=[row, row, row,
                  pl.BlockSpec((bm, 2), lambda i: (i, 0)),
                  pl.BlockSpec((1, 1, d), lambda i: (i // per_b, 0, 0)),
                  pl.BlockSpec((1, d), lambda i: (0, 0))],
        out_specs=row,
        compiler_params=_cparams(("parallel",)),
        name="combine_final" if final else "combine",
    )(x, y1, y2, w12, g2, fg)


def _rope_tables(seq):
    half = MLA_ROPE // 2
    inv = ROPE_THETA ** (-jnp.arange(half, dtype=F32) * 2.0 / MLA_ROPE)
    ang = jnp.arange(seq, dtype=F32)[:, None] * inv[None, :]
    cos, sin = jnp.cos(ang), jnp.sin(ang)
    zeros = jnp.zeros((seq, LANES - 2 * half), F32)
    z32 = jnp.zeros((seq, half), F32)
    cos_t = jnp.concatenate([cos, cos, zeros], axis=1)
    sin_up = jnp.concatenate([z32, sin, zeros], axis=1)
    sin_dn = jnp.concatenate([-sin, z32, zeros], axis=1)
    return cos_t, sin_up, sin_dn


def _pack_w_in(w):
    d = w.shape[0]
    o1 = MLA_Q_LORA
    o2 = o1 + MLA_KV_LORA
    o3 = o2 + MLA_ROPE
    pad = jnp.zeros((d, PROJ_COLS - COL_KR - MLA_ROPE), w.dtype)
    return jnp.concatenate([w[:, o3:], w[:, :o1], w[:, o1:o2], w[:, o2:o3], pad], axis=1).astype(BF16)


def _pack_w_uq(w):
    r = w.shape[0]
    w = w.reshape(r, MLA_HEADS, MLA_NOPE + MLA_ROPE)
    w = jnp.pad(w, ((0, 0), (0, 0), (0, MLA_QK_PAD - MLA_NOPE - MLA_ROPE)))
    return w.reshape(r, MLA_HEADS * MLA_QK_PAD).astype(BF16)


def _dispatch(info, counts, t):
    e1 = info[0].astype(jnp.int32)
    e2 = info[1].astype(jnp.int32)
    r1 = info[2].astype(jnp.int32)
    r2 = info[3].astype(jnp.int32)
    cnt = counts[:, 0].astype(jnp.int32)
    padded = (cnt + MOE_BLOCK - 1) // MOE_BLOCK * MOE_BLOCK
    pad_end = jnp.cumsum(padded)
    pad_start = pad_end - padded
    d1 = pad_start[e1] + r1
    d2 = pad_start[e2] + r2
    n_rows = 2 * t + N_EXPERTS * MOE_BLOCK
    n_blocks = n_rows // MOE_BLOCK
    tok = jnp.arange(t, dtype=jnp.int32)
    row_tok = jnp.zeros((n_rows,), jnp.int32).at[d1].set(tok).at[d2].set(tok)
    blk_e = jnp.minimum(jnp.searchsorted(pad_end, jnp.arange(n_blocks, dtype=jnp.int32) * MOE_BLOCK, side='right'),
                        N_EXPERTS - 1).astype(jnp.int32)
    n_used = (pad_end[-1] // MOE_BLOCK).astype(jnp.int32).reshape(1)
    return d1, d2, row_tok, blk_e, n_used


def kernel(x, c, ada_w, ada_b, norm1_g, norm2_g, w_in, q_norm_g, kv_norm_g, w_uq, w_ukv, out_norm_a_g,
           out_norm_b_g, w_out, router_w, router_b, exp_gate, exp_up, exp_down, final_g):
    batch, seq, d = x.shape
    t = batch * seq
    depth = ada_w.shape[0]
    xf = x.reshape(t, d)

    c_pad = jnp.pad(c, ((0, 8 - batch), (0, 0)))
    mod = ada_mod(c_pad, ada_w, ada_b)[:, :batch]
    cos_t, sin_up, sin_dn = _rope_tables(seq)

    rw_pad = jnp.pad(router_w, ((0, 0), (0, LANES - N_EXPERTS)))
    rw_hi = rw_pad.astype(BF16)
    rw_lo = (rw_pad - rw_hi.astype(F32)).astype(BF16)
    rb = router_b.reshape(N_EXPERTS, 1)
    rbm = 512
    tri = (jnp.arange(rbm)[:, None] < jnp.arange(rbm)[None, :]).astype(BF16)

    for l in range(depth):
        sh1, sc1, g1, sh2, sc2, g2 = [m.reshape(batch, 1, d) for m in jnp.split(mod[l], N_ADA, axis=-1)]
        proj = norm_matmul(xf, norm1_g[l].reshape(1, d), sc1, sh1, _pack_w_in(w_in[l]), seq)
        q, k, v = mla_proj(proj, q_norm_g[l].reshape(1, -1), kv_norm_g[l].reshape(1, -1),
                           _pack_w_uq(w_uq[l]), w_ukv[l].astype(BF16), cos_t, sin_up, sin_dn, seq)
        oa = mla_attn(q, k, v, batch, seq)
        obs, lses = zip(*[dil_attn(proj, batch, seq, rate) for rate in DIL_RATES])
        xf = out_proj(oa, obs, lses, out_norm_a_g[l].reshape(1, -1), out_norm_b_g[l].reshape(1, -1),
                      w_out[l].astype(BF16), xf, g1, seq)

        h2, info, counts = router(xf, norm2_g[l].reshape(1, d), sc2, sh2, rw_hi, rw_lo, rb, tri, seq)
        d1, d2, row_tok, blk_e, n_used = _dispatch(info, counts, t)
        xs = jnp.take(h2, row_tok, axis=0)
        wgu = jnp.concatenate([exp_gate[l], exp_up[l]], axis=-1).astype(BF16)
        y = moe_ffn(blk_e, n_used, xs, wgu, exp_down[l].astype(BF16))
        y1 = jnp.take(y, d1, axis=0)
        y2 = jnp.take(y, d2, axis=0)
        w12 = jnp.stack([info[4], info[5]], axis=1)
        xf = combine(xf, y1, y2, w12, g2, final_g.reshape(1, d), seq, final=(l == depth - 1))
    return xf.reshape(batch, seq, d)
```

```python
import functools

import jax
import jax.numpy as jnp
import numpy as np
from jax import lax
from jax.experimental import pallas as pl
from jax.experimental.pallas import tpu as pltpu

F32 = jnp.float32
BF16 = jnp.bfloat16

D_MODEL = 2048
DEPTH = 2
MLA_HEADS = 8
MLA_NOPE = 128
MLA_ROPE = 64
MLA_V = 128
MLA_Q_LORA = D_MODEL // 4
MLA_KV_LORA = D_MODEL // 8
ROPE_THETA = 10000.0
DIL_HEADS = 8
DIL_HEAD_DIM = 128
DIL_WINDOWS = (128, 512, 2048)
DIL_RATES = (1, 4, 16)
DIL_WIDTH = DIL_HEADS * DIL_HEAD_DIM
MLA_WIDTH = MLA_HEADS * MLA_V
N_EXPERTS = 16
N_GROUPS = 4
GROUP_SIZE = N_EXPERTS // N_GROUPS
D_EXPERT = D_MODEL // 2
N_ADA = 6
EPS = 1e-6
LOG2_E = 1.4426950408889634

LANES = 128
MLA_QK_PAD = 2 * LANES
PROJ_COLS = 4096
COL_Q_B, COL_K_B, COL_V_B = 0, DIL_WIDTH, 2 * DIL_WIDTH
COL_CQ = 3 * DIL_WIDTH
COL_CKV = COL_CQ + MLA_Q_LORA
COL_KR = COL_CKV + MLA_KV_LORA
BAND = 128
MOE_BLOCK = 512
VMEM_LIMIT = 56 * 1024 * 1024


def _cparams(sem):
    return pltpu.CompilerParams(dimension_semantics=sem, vmem_limit_bytes=VMEM_LIMIT)


def _ada_kernel(c_ref, w_ref, b_ref, o_ref):
    c = c_ref[...]
    act = (c * jax.nn.sigmoid(c)).astype(BF16)
    o_ref[0] = jnp.dot(act, w_ref[0].astype(BF16), preferred_element_type=F32) + b_ref[0]


def ada_mod(c_pad, ada_w, ada_b):
    depth, d, n = ada_w.shape
    rows = c_pad.shape[0]
    bn = 1024
    return pl.pallas_call(
        _ada_kernel,
        out_shape=jax.ShapeDtypeStruct((depth, rows, n), F32),
        grid=(depth, n // bn),
        in_specs=[
            pl.BlockSpec((rows, d), lambda l, j: (0, 0)),
            pl.BlockSpec((1, d, bn), lambda l, j: (l, 0, j)),
            pl.BlockSpec((1, 1, bn), lambda l, j: (l, 0, j)),
        ],
        out_specs=pl.BlockSpec((1, rows, bn), lambda l, j: (l, 0, j)),
        compiler_params=_cparams(("parallel", "parallel")),
        name="ada_mod",
    )(c_pad, ada_w, ada_b.reshape(depth, 1, n))


def _modulated_norm(x, g, sc, sh):
    ms = jnp.mean(x * x, axis=-1, keepdims=True)
    return (x * lax.rsqrt(ms + EPS) * g) * (1.0 + sc) + sh


def _norm_matmul_kernel(x_ref, g_ref, sc_ref, sh_ref, w_ref, o_ref, h_scr):
    @pl.when(pl.program_id(1) == 0)
    def _():
        h_scr[...] = _modulated_norm(x_ref[...], g_ref[...], sc_ref[0], sh_ref[0]).astype(BF16)

    o_ref[...] = jnp.dot(h_scr[...], w_ref[...], preferred_element_type=F32).astype(o_ref.dtype)


def norm_matmul(x, g, sc, sh, w, seq):
    t, d = x.shape
    n = w.shape[1]
    bm = min(1024, seq)
    bn = 1024
    per_b = seq // bm
    return pl.pallas_call(
        _norm_matmul_kernel,
        out_shape=jax.ShapeDtypeStruct((t, n), BF16),
        grid=(t // bm, n // bn),
        in_specs=[
            pl.BlockSpec((bm, d), lambda i, j: (i, 0)),
            pl.BlockSpec((1, d), lambda i, j: (0, 0)),
            pl.BlockSpec((1, 1, d), lambda i, j: (i // per_b, 0, 0)),
            pl.BlockSpec((1, 1, d), lambda i, j: (i // per_b, 0, 0)),
            pl.BlockSpec((d, bn), lambda i, j: (0, j)),
        ],
        out_specs=pl.BlockSpec((bm, bn), lambda i, j: (i, j)),
        scratch_shapes=[pltpu.VMEM((bm, d), BF16)],
        compiler_params=_cparams(("parallel", "arbitrary")),
        name="norm_matmul",
    )(x, g, sc, sh, w)


def _rms(x, g):
    ms = jnp.mean(x * x, axis=-1, keepdims=True)
    return x * lax.rsqrt(ms + EPS) * g


def _rope_lanes(x, cos, sin_up, sin_dn):
    return x * cos + pltpu.roll(x, 32, 1) * sin_up + pltpu.roll(x, 96, 1) * sin_dn


def _mla_proj_kernel(cq_ref, ckv_ref, kr_ref, qg_ref, kvg_ref, wuqt_ref, wuk_ref, wuvt_ref,
                     cos_ref, sup_ref, sdn_ref, cost_ref, sint_ref, qt_ref, k_ref, vt_ref):
    nt = (((1,), (1,)), ((), ()))
    scale = (MLA_NOPE + MLA_ROPE) ** -0.5 * LOG2_E
    half = MLA_ROPE // 2

    cqn = _rms(cq_ref[...].astype(F32), qg_ref[...]).astype(BF16)
    qt = lax.dot_general(wuqt_ref[...], cqn, nt, preferred_element_type=F32) * scale
    cost, sint = cost_ref[...], sint_ref[...]
    for h in range(MLA_HEADS):
        r0 = h * MLA_QK_PAD
        r1 = r0 + MLA_NOPE
        qt_ref[r0:r1, :] = qt[r0:r1].astype(BF16)
        x1, x2 = qt[r1:r1 + half], qt[r1 + half:r1 + 2 * half]
        qt_ref[r1:r1 + half, :] = (x1 * cost - x2 * sint).astype(BF16)
        qt_ref[r1 + half:r1 + 2 * half, :] = (x2 * cost + x1 * sint).astype(BF16)
        qt_ref[r1 + 2 * half:r0 + MLA_QK_PAD, :] = jnp.zeros((MLA_QK_PAD - MLA_NOPE - 2 * half, qt.shape[1]), BF16)

    ckvn = _rms(ckv_ref[...].astype(F32), kvg_ref[...]).astype(BF16)
    kn = jnp.dot(ckvn, wuk_ref[...], preferred_element_type=F32)
    krr = _rope_lanes(kr_ref[...].astype(F32), cos_ref[...], sup_ref[...], sdn_ref[...]).astype(BF16)
    for h in range(MLA_HEADS):
        lo = h * MLA_QK_PAD
        k_ref[:, lo:lo + LANES] = kn[:, h * MLA_NOPE:(h + 1) * MLA_NOPE].astype(BF16)
        k_ref[:, lo + LANES:lo + 2 * LANES] = krr
    vt_ref[...] = lax.dot_general(wuvt_ref[...], ckvn, nt, preferred_element_type=F32).astype(BF16)


def mla_proj(proj, qg, kvg, wuqt, wuk, wuvt, tables, seq):
    t = proj.shape[0]
    bm = 512
    per_b = seq // bm
    width = MLA_HEADS * MLA_QK_PAD
    half = MLA_ROPE // 2
    cos, sup, sdn, cost, sint = tables
    lane_tab = pl.BlockSpec((bm, LANES), lambda i: (i % per_b, 0))
    row_tab = pl.BlockSpec((half, bm), lambda i: (0, i % per_b))
    return pl.pallas_call(
        _mla_proj_kernel,
        out_shape=(jax.ShapeDtypeStruct((width, t), BF16),
                   jax.ShapeDtypeStruct((t, width), BF16),
                   jax.ShapeDtypeStruct((MLA_WIDTH, t), BF16)),
        grid=(t // bm,),
        in_specs=[
            pl.BlockSpec((bm, MLA_Q_LORA), lambda i: (i, COL_CQ // MLA_Q_LORA)),
            pl.BlockSpec((bm, MLA_KV_LORA), lambda i: (i, COL_CKV // MLA_KV_LORA)),
            pl.BlockSpec((bm, LANES), lambda i: (i, COL_KR // LANES)),
            pl.BlockSpec((1, MLA_Q_LORA), lambda i: (0, 0)),
            pl.BlockSpec((1, MLA_KV_LORA), lambda i: (0, 0)),
            pl.BlockSpec((width, MLA_Q_LORA), lambda i: (0, 0)),
            pl.BlockSpec((MLA_KV_LORA, MLA_HEADS * MLA_NOPE), lambda i: (0, 0)),
            pl.BlockSpec((MLA_WIDTH, MLA_KV_LORA), lambda i: (0, 0)),
            lane_tab, lane_tab, lane_tab, row_tab, row_tab,
        ],
        out_specs=(pl.BlockSpec((width, bm), lambda i: (0, i)),
                   pl.BlockSpec((bm, width), lambda i: (i, 0)),
                   pl.BlockSpec((MLA_WIDTH, bm), lambda i: (0, i))),
        compiler_params=_cparams(("parallel",)),
        name="mla_proj",
    )(proj, proj, proj, qg, kvg, wuqt, wuk, wuvt, cos, sup, sdn, cost, sint)


def _mla_attn_kernel(qt_ref, k_ref, vt_ref, o_ref, *, blk):
    i = pl.program_id(2)
    halves = (qt_ref[:, 0:blk], qt_ref[:, blk:2 * blk])

    def tile(qt, j, carry, masked):
        m, l, acc = carry
        start = pl.multiple_of(j * blk, blk)
        k = k_ref[pl.ds(start, blk), :]
        vt = vt_ref[:, pl.ds(start, blk)]
        s = jnp.dot(k, qt, preferred_element_type=F32)
        if masked:
            key = lax.broadcasted_iota(jnp.int32, s.shape, 0)
            qry = lax.broadcasted_iota(jnp.int32, s.shape, 1)
            s = jnp.where(key <= qry, s, -jnp.inf)
        m_new = jnp.maximum(m, jnp.max(s, axis=0, keepdims=True))
        alpha = jnp.exp2(m - m_new)
        p = jnp.exp2(s - m_new)
        l = alpha * l + jnp.sum(p, axis=0, keepdims=True)
        acc = alpha * acc + jnp.dot(vt, p.astype(BF16), preferred_element_type=F32)
        return m_new, l, acc

    init = (jnp.full((1, blk), -jnp.inf, F32), jnp.zeros((1, blk), F32), jnp.zeros((MLA_V, blk), F32))

    def body(j, carry):
        return tuple(tile(qt, j, c, False) for qt, c in zip(halves, carry))

    ca, cb = lax.fori_loop(0, 2 * i, body, (init, init))
    ca = tile(halves[0], 2 * i, ca, True)
    cb = tile(halves[1], 2 * i, cb, False)
    cb = tile(halves[1], 2 * i + 1, cb, True)
    for n, (_, l, acc) in enumerate((ca, cb)):
        o_ref[n * blk:(n + 1) * blk, :] = (acc / l).T.astype(o_ref.dtype)


def mla_attn(qt, k, vt, batch, seq):
    t = k.shape[0]
    blk = 512
    nq = seq // (2 * blk)
    return pl.pallas_call(
        functools.partial(_mla_attn_kernel, blk=blk),
        out_shape=jax.ShapeDtypeStruct((t, MLA_WIDTH), BF16),
        grid=(batch, MLA_HEADS, nq),
        in_specs=[
            pl.BlockSpec((MLA_QK_PAD, 2 * blk), lambda b, h, i: (h, b * nq + i)),
            pl.BlockSpec((seq, MLA_QK_PAD), lambda b, h, i: (b, h)),
            pl.BlockSpec((MLA_V, seq), lambda b, h, i: (h, b)),
        ],
        out_specs=pl.BlockSpec((2 * blk, MLA_V), lambda b, h, i: (b * nq + i, h)),
        compiler_params=_cparams(("parallel", "parallel", "arbitrary")),
        name="mla_attn",
    )(qt, k, vt)


def _alibi_slope(h):
    return 2.0 ** (-8.0 * (h + 1) / DIL_HEADS)


def _dil_attn_kernel(q_ref, k_ref, v_ref, kp_ref, vp_ref, o_ref, lse_ref, kc_scr, vc_scr, *, rate, rows):
    n = pl.program_id(2)
    kc_scr[0:BAND, :] = kp_ref[0]
    kc_scr[BAND:BAND + rows, :] = k_ref[0]
    vc_scr[0:BAND, :] = vp_ref[0]
    vc_scr[BAND:BAND + rows, :] = v_ref[0]

    qi = lax.broadcasted_iota(jnp.int32, (BAND, 2 * BAND), 0)
    kj = lax.broadcasted_iota(jnp.int32, (BAND, 2 * BAND), 1)
    delta = qi + BAND - kj
    in_band = (delta >= 0) & (delta <= BAND)
    dist = (rate * delta).astype(F32)
    scale = DIL_HEAD_DIM ** -0.5
    sub_per_step = rows // BAND

    def sub_block(t, _):
        r0 = pl.multiple_of(t * BAND, BAND)
        first = (n * sub_per_step + t) == 0
        valid = in_band & jnp.logical_or(jnp.logical_not(first), kj >= BAND)
        for h in range(DIL_HEADS):
            c0 = h * DIL_HEAD_DIM
            q = q_ref[0, pl.ds(r0, BAND), c0:c0 + DIL_HEAD_DIM]
            k = kc_scr[pl.ds(r0, 2 * BAND), c0:c0 + DIL_HEAD_DIM]
            v = vc_scr[pl.ds(r0, 2 * BAND), c0:c0 + DIL_HEAD_DIM]
            s = lax.dot_general(q, k, (((1,), (1,)), ((), ())), preferred_element_type=F32)
            s = jnp.where(valid, s * scale - _alibi_slope(h) * dist, -jnp.inf)
            m = jnp.max(s, axis=-1, keepdims=True)
            p = jnp.exp(s - m)
            l = jnp.sum(p, axis=-1, keepdims=True)
            o = jnp.dot(p.astype(BF16), v, preferred_element_type=F32) / l
            o_ref[0, pl.ds(r0, BAND), c0:c0 + DIL_HEAD_DIM] = o.astype(o_ref.dtype)
            lse_ref[0, 0, pl.ds(r0, BAND), h:h + 1] = m + jnp.log(l)
        return 0

    lax.fori_loop(0, sub_per_step, sub_block, 0)


def dil_attn(proj, batch, seq, rate):
    n_sub = seq // rate
    rows = min(n_sub, 1024)
    n_blk = n_sub // rows
    tiles = PROJ_COLS // DIL_WIDTH
    pv = proj.reshape(batch, n_sub, rate * PROJ_COLS)
    prev = lambda b, r, n: jnp.maximum(n * (rows // BAND) - 1, 0)
    o, lse = pl.pallas_call(
        functools.partial(_dil_attn_kernel, rate=rate, rows=rows),
        out_shape=(jax.ShapeDtypeStruct((batch, n_sub, rate * DIL_WIDTH), BF16),
                   jax.ShapeDtypeStruct((batch, rate, n_sub, DIL_HEADS), F32)),
        grid=(batch, rate, n_blk),
        in_specs=[
            pl.BlockSpec((1, rows, DIL_WIDTH), lambda b, r, n: (b, n, r * tiles + COL_Q_B // DIL_WIDTH)),
            pl.BlockSpec((1, rows, DIL_WIDTH), lambda b, r, n: (b, n, r * tiles + COL_K_B // DIL_WIDTH)),
            pl.BlockSpec((1, rows, DIL_WIDTH), lambda b, r, n: (b, n, r * tiles + COL_V_B // DIL_WIDTH)),
            pl.BlockSpec((1, BAND, DIL_WIDTH), lambda b, r, n: (b, prev(b, r, n), r * tiles + COL_K_B // DIL_WIDTH)),
            pl.BlockSpec((1, BAND, DIL_WIDTH), lambda b, r, n: (b, prev(b, r, n), r * tiles + COL_V_B // DIL_WIDTH)),
        ],
        out_specs=(pl.BlockSpec((1, rows, DIL_WIDTH), lambda b, r, n: (b, n, r)),
                   pl.BlockSpec((1, 1, rows, DIL_HEADS), lambda b, r, n: (b, r, n, 0))),
        scratch_shapes=[pltpu.VMEM((rows + BAND, DIL_WIDTH), BF16), pltpu.VMEM((rows + BAND, DIL_WIDTH), BF16)],
        compiler_params=_cparams(("parallel", "parallel", "arbitrary")),
        name=f"dil_attn_r{rate}",
    )(pv, pv, pv, pv, pv)
    o = o.reshape(batch * seq, DIL_WIDTH)
    lse = lse.transpose(0, 2, 1, 3).reshape(batch * seq, DIL_HEADS)
    return o, lse


def _out_proj_kernel(oa_ref, o1_ref, o2_ref, o3_ref, l1_ref, l2_ref, l3_ref, ga_ref, gb_ref,
                     w_ref, x_ref, g1_ref, xo_ref, y_scr):
    @pl.when(pl.program_id(1) == 0)
    def _():
        l1, l2, l3 = l1_ref[...], l2_ref[...], l3_ref[...]
        mx = jnp.maximum(jnp.maximum(l1, l2), l3)
        e1, e2, e3 = jnp.exp(l1 - mx), jnp.exp(l2 - mx), jnp.exp(l3 - mx)
        inv = 1.0 / (e1 + e2 + e3)
        a1, a2, a3 = e1 * inv, e2 * inv, e3 * inv
        parts = []
        for h in range(DIL_HEADS):
            c = slice(h * DIL_HEAD_DIM, (h + 1) * DIL_HEAD_DIM)
            parts.append(a1[:, h:h + 1] * o1_ref[:, c].astype(F32)
                         + a2[:, h:h + 1] * o2_ref[:, c].astype(F32)
                         + a3[:, h:h + 1] * o3_ref[:, c].astype(F32))
        ob = jnp.concatenate(parts, axis=-1)
        y_scr[:, 0:MLA_WIDTH] = _rms(oa_ref[...].astype(F32), ga_ref[...]).astype(BF16)
        y_scr[:, MLA_WIDTH:MLA_WIDTH + DIL_WIDTH] = _rms(ob, gb_ref[...]).astype(BF16)

    y = jnp.dot(y_scr[...], w_ref[...], preferred_element_type=F32)
    xo_ref[...] = x_ref[...] + g1_ref[0] * y


def out_proj(oa, obs, lses, ga, gb, w, x, g1, seq):
    t, d = x.shape
    bm = 512
    bn = 1024
    per_b = seq // bm
    row = lambda width: pl.BlockSpec((bm, width), lambda i, j: (i, 0))
    return pl.pallas_call(
        _out_proj_kernel,
        out_shape=jax.ShapeDtypeStruct((t, d), F32),
        grid=(t // bm, d // bn),
        in_specs=[
            row(MLA_WIDTH), row(DIL_WIDTH), row(DIL_WIDTH), row(DIL_WIDTH),
            row(DIL_HEADS), row(DIL_HEADS), row(DIL_HEADS),
            pl.BlockSpec((1, MLA_WIDTH), lambda i, j: (0, 0)),
            pl.BlockSpec((1, DIL_WIDTH), lambda i, j: (0, 0)),
            pl.BlockSpec((MLA_WIDTH + DIL_WIDTH, bn), lambda i, j: (0, j)),
            pl.BlockSpec((bm, bn), lambda i, j: (i, j)),
            pl.BlockSpec((1, 1, bn), lambda i, j: (i // per_b, 0, j)),
        ],
        out_specs=pl.BlockSpec((bm, bn), lambda i, j: (i, j)),
        scratch_shapes=[pltpu.VMEM((bm, MLA_WIDTH + DIL_WIDTH), BF16)],
        compiler_params=_cparams(("parallel", "arbitrary")),
        name="out_proj",
    )(oa, *obs, *lses, ga, gb, w, x, g1)


def _top2_sum(a, b, c, d):
    hi1, lo1 = jnp.maximum(a, b), jnp.minimum(a, b)
    hi2, lo2 = jnp.maximum(c, d), jnp.minimum(c, d)
    return jnp.maximum(hi1, hi2) + jnp.maximum(jnp.minimum(hi1, hi2), jnp.maximum(lo1, lo2))


def _router_kernel(x_ref, g_ref, sc_ref, sh_ref, rw_hi_ref, rw_lo_ref, rb_ref, tri_ref,
                   h_ref, info_ref, cnt_ref, base_scr):
    @pl.when(pl.program_id(0) == 0)
    def _():
        base_scr[...] = jnp.zeros_like(base_scr)

    h = _modulated_norm(x_ref[...], g_ref[...], sc_ref[0], sh_ref[0])
    h_hi = h.astype(BF16)
    h_ref[...] = h_hi
    h_lo = (h - h_hi.astype(F32)).astype(BF16)
    rw_hi = rw_hi_ref[...]
    logits = (jnp.dot(h_hi, rw_hi, preferred_element_type=F32)
              + jnp.dot(h_lo, rw_hi, preferred_element_type=F32)
              + jnp.dot(h_hi, rw_lo_ref[...], preferred_element_type=F32))
    lt = logits.T[0:N_EXPERTS, :]
    score = jax.nn.sigmoid(lt)
    biased = score + rb_ref[...]
    srow = [score[e:e + 1, :] for e in range(N_EXPERTS)]
    brow = [biased[e:e + 1, :] for e in range(N_EXPERTS)]

    gsum = [_top2_sum(*brow[GROUP_SIZE * g:GROUP_SIZE * (g + 1)]) for g in range(N_GROUPS)]
    best, gsel = gsum[0], jnp.zeros_like(gsum[0], dtype=jnp.int32)
    for g in range(1, N_GROUPS):
        better = gsum[g] > best
        gsel = jnp.where(better, g, gsel)
        best = jnp.where(better, gsum[g], best)

    def pick(rows, j):
        out = rows[j]
        for g in range(1, N_GROUPS):
            out = jnp.where(gsel == g, rows[GROUP_SIZE * g + j], out)
        return out

    bv = [pick(brow, j) for j in range(GROUP_SIZE)]
    sv = [pick(srow, j) for j in range(GROUP_SIZE)]
    v1, i1, s1 = bv[0], jnp.zeros_like(gsel), sv[0]
    for j in range(1, GROUP_SIZE):
        better = bv[j] > v1
        i1 = jnp.where(better, j, i1)
        s1 = jnp.where(better, sv[j], s1)
        v1 = jnp.where(better, bv[j], v1)
    v2 = jnp.full_like(v1, -jnp.inf)
    i2, s2 = jnp.zeros_like(gsel), jnp.zeros_like(s1)
    for j in range(GROUP_SIZE):
        better = (i1 != j) & (bv[j] > v2)
        i2 = jnp.where(better, j, i2)
        s2 = jnp.where(better, sv[j], s2)
        v2 = jnp.where(better, bv[j], v2)
    e1 = gsel * GROUP_SIZE + i1
    e2 = gsel * GROUP_SIZE + i2
    wsum = s1 + s2

    eid = lax.broadcasted_iota(jnp.int32, (N_EXPERTS, e1.shape[1]), 0)
    hit1 = eid == e1
    hit2 = eid == e2
    onehot = jnp.where(hit1 | hit2, 1.0, 0.0)
    prefix = jnp.dot(onehot.astype(BF16), tri_ref[...], preferred_element_type=F32) + base_scr[:, 0:1]
    r1 = jnp.sum(jnp.where(hit1, prefix, 0.0), axis=0, keepdims=True)
    r2 = jnp.sum(jnp.where(hit2, prefix, 0.0), axis=0, keepdims=True)
    new_base = base_scr[...] + jnp.sum(onehot, axis=1, keepdims=True)
    base_scr[...] = new_base
    cnt_ref[...] = new_base

    rows = (e1.astype(F32), e2.astype(F32), r1, r2, s1 / wsum, s2 / wsum, jnp.zeros_like(s1), jnp.zeros_like(s1))
    for r, val in enumerate(rows):
        info_ref[r:r + 1, :] = val


def router(x, g, sc, sh, rw_hi, rw_lo, rb, tri, seq):
    t, d = x.shape
    bm = tri.shape[0]
    per_b = seq // bm
    return pl.pallas_call(
        _router_kernel,
        out_shape=(jax.ShapeDtypeStruct((t, d), BF16),
                   jax.ShapeDtypeStruct((8, t), F32),
                   jax.ShapeDtypeStruct((N_EXPERTS, LANES), F32)),
        grid=(t // bm,),
        in_specs=[
            pl.BlockSpec((bm, d), lambda i: (i, 0)),
            pl.BlockSpec((1, d), lambda i: (0, 0)),
            pl.BlockSpec((1, 1, d), lambda i: (i // per_b, 0, 0)),
            pl.BlockSpec((1, 1, d), lambda i: (i // per_b, 0, 0)),
            pl.BlockSpec((d, LANES), lambda i: (0, 0)),
            pl.BlockSpec((d, LANES), lambda i: (0, 0)),
            pl.BlockSpec((N_EXPERTS, 1), lambda i: (0, 0)),
            pl.BlockSpec((bm, bm), lambda i: (0, 0)),
        ],
        out_specs=(pl.BlockSpec((bm, d), lambda i: (i, 0)),
                   pl.BlockSpec((8, bm), lambda i: (0, i)),
                   pl.BlockSpec((N_EXPERTS, LANES), lambda i: (0, 0))),
        scratch_shapes=[pltpu.VMEM((N_EXPERTS, LANES), F32)],
        compiler_params=_cparams(("arbitrary",)),
        name="router",
    )(x, g, sc, sh, rw_hi, rw_lo, rb, tri)


def _moe_ffn_kernel(blk_e_ref, n_used_ref, x_ref, wgu_ref, wd_ref, o_ref):
    used = pl.program_id(0) < n_used_ref[0]

    @pl.when(jnp.logical_not(used))
    def _():
        o_ref[...] = jnp.zeros_like(o_ref)

    @pl.when(used)
    def _():
        gu = jnp.dot(x_ref[...], wgu_ref[0], preferred_element_type=F32)
        a, u = gu[:, 0:D_EXPERT], gu[:, D_EXPERT:2 * D_EXPERT]
        mid = (a * jax.nn.sigmoid(a) * u).astype(BF16)
        o_ref[...] = jnp.dot(mid, wd_ref[0], preferred_element_type=F32).astype(o_ref.dtype)


def moe_ffn(blk_e, n_used, xs, wgu, wd):
    n_rows, d = xs.shape
    n_blocks = n_rows // MOE_BLOCK
    return pl.pallas_call(
        _moe_ffn_kernel,
        out_shape=jax.ShapeDtypeStruct((n_rows, d), BF16),
        grid_spec=pltpu.PrefetchScalarGridSpec(
            num_scalar_prefetch=2,
            grid=(n_blocks,),
            in_specs=[
                pl.BlockSpec((MOE_BLOCK, d), lambda i, be, nu: (i, 0)),
                pl.BlockSpec((1, d, 2 * D_EXPERT), lambda i, be, nu: (be[i], 0, 0)),
                pl.BlockSpec((1, D_EXPERT, d), lambda i, be, nu: (be[i], 0, 0)),
            ],
            out_specs=pl.BlockSpec((MOE_BLOCK, d), lambda i, be, nu: (i, 0)),
        ),
        compiler_params=_cparams(("arbitrary",)),
        name="moe_ffn",
    )(blk_e, n_used, xs, wgu, wd)


def _combine_kernel(x_ref, y1_ref, y2_ref, w_ref, g2_ref, fg_ref, o_ref, *, final):
    w = w_ref[...]
    moe = y1_ref[...].astype(F32) * w[:, 0:1] + y2_ref[...].astype(F32) * w[:, 1:2]
    x = x_ref[...] + g2_ref[0] * moe
    o_ref[...] = _rms(x, fg_ref[...]) if final else x


def combine(x, y1, y2, w12, g2, fg, seq, final):
    t, d = x.shape
    bm = 512
    per_b = seq // bm
    row = pl.BlockSpec((bm, d), lambda i: (i, 0))
    return pl.pallas_call(
        functools.partial(_combine_kernel, final=final),
        out_shape=jax.ShapeDtypeStruct((t, d), F32),
        grid=(t // bm,),
        in_specs=[row, row, row,
                  pl.BlockSpec((bm, 2), lambda i: (i, 0)),
                  pl.BlockSpec((1, 1, d), lambda i: (i // per_b, 0, 0)),
                  pl.BlockSpec((1, d), lambda i: (0, 0))],
        out_specs=row,
        compiler_params=_cparams(("parallel",)),
        name="combine_final" if final else "combine",
    )(x, y1, y2, w12, g2, fg)


def _rope_tables(seq):
    half = MLA_ROPE // 2
    inv = ROPE_THETA ** (-jnp.arange(half, dtype=F32) * 2.0 / MLA_ROPE)
    ang = jnp.arange(seq, dtype=F32)[:, None] * inv[None, :]
    cos, sin = jnp.cos(ang), jnp.sin(ang)
    zeros = jnp.zeros((seq, LANES - 2 * half), F32)
    z32 = jnp.zeros((seq, half), F32)
    cos_t = jnp.concatenate([cos, cos, zeros], axis=1)
    sin_up = jnp.concatenate([z32, sin, zeros], axis=1)
    sin_dn = jnp.concatenate([-sin, z32, zeros], axis=1)
    return cos_t, sin_up, sin_dn, cos.T, sin.T


def _pack_w_in(w):
    d = w.shape[0]
    o1 = MLA_Q_LORA
    o2 = o1 + MLA_KV_LORA
    o3 = o2 + MLA_ROPE
    pad = jnp.zeros((d, PROJ_COLS - COL_KR - MLA_ROPE), w.dtype)
    return jnp.concatenate([w[:, o3:], w[:, :o1], w[:, o1:o2], w[:, o2:o3], pad], axis=1).astype(BF16)


def _pack_w_uq_t(w):
    r = w.shape[0]
    w = w.reshape(r, MLA_HEADS, MLA_NOPE + MLA_ROPE)
    w = jnp.pad(w, ((0, 0), (0, 0), (0, MLA_QK_PAD - MLA_NOPE - MLA_ROPE)))
    return w.reshape(r, MLA_HEADS * MLA_QK_PAD).T.astype(BF16)


def _split_w_ukv(w):
    r = w.shape[0]
    w = w.reshape(r, MLA_HEADS, MLA_NOPE + MLA_V)
    wuk = w[:, :, :MLA_NOPE].reshape(r, MLA_HEADS * MLA_NOPE).astype(BF16)
    wuvt = w[:, :, MLA_NOPE:].reshape(r, MLA_WIDTH).T.astype(BF16)
    return wuk, wuvt


def _dispatch(info, counts, t):
    e1 = info[0].astype(jnp.int32)
    e2 = info[1].astype(jnp.int32)
    r1 = info[2].astype(jnp.int32)
    r2 = info[3].astype(jnp.int32)
    cnt = counts[:, 0].astype(jnp.int32)
    padded = (cnt + MOE_BLOCK - 1) // MOE_BLOCK * MOE_BLOCK
    pad_end = jnp.cumsum(padded)
    pad_start = pad_end - padded
    d1 = pad_start[e1] + r1
    d2 = pad_start[e2] + r2
    n_rows = 2 * t + N_EXPERTS * MOE_BLOCK
    n_blocks = n_rows // MOE_BLOCK
    tok = jnp.arange(t, dtype=jnp.int32)
    row_tok = jnp.zeros((n_rows,), jnp.int32).at[d1].set(tok).at[d2].set(tok)
    blk_e = jnp.minimum(jnp.searchsorted(pad_end, jnp.arange(n_blocks, dtype=jnp.int32) * MOE_BLOCK, side='right'),
                        N_EXPERTS - 1).astype(jnp.int32)
    n_used = (pad_end[-1] // MOE_BLOCK).astype(jnp.int32).reshape(1)
    return d1, d2, row_tok, blk_e, n_used


def kernel(x, c, ada_w, ada_b, norm1_g, norm2_g, w_in, q_norm_g, kv_norm_g, w_uq, w_ukv, out_norm_a_g,
           out_norm_b_g, w_out, router_w, router_b, exp_gate, exp_up, exp_down, final_g):
    batch, seq, d = x.shape
    t = batch * seq
    depth = ada_w.shape[0]
    xf = x.reshape(t, d)

    c_pad = jnp.pad(c, ((0, 8 - batch), (0, 0)))
    mod = ada_mod(c_pad, ada_w, ada_b)[:, :batch]
    tables = _rope_tables(seq)

    rw_pad = jnp.pad(router_w, ((0, 0), (0, LANES - N_EXPERTS)))
    rw_hi = rw_pad.astype(BF16)
    rw_lo = (rw_pad - rw_hi.astype(F32)).astype(BF16)
    rb = router_b.reshape(N_EXPERTS, 1)
    rbm = 512
    tri = (jnp.arange(rbm)[:, None] < jnp.arange(rbm)[None, :]).astype(BF16)

    for l in range(depth):
        sh1, sc1, g1, sh2, sc2, g2 = [m.reshape(batch, 1, d) for m in jnp.split(mod[l], N_ADA, axis=-1)]
        proj = norm_matmul(xf, norm1_g[l].reshape(1, d), sc1, sh1, _pack_w_in(w_in[l]), seq)
        wuk, wuvt = _split_w_ukv(w_ukv[l])
        qt, k, vt = mla_proj(proj, q_norm_g[l].reshape(1, -1), kv_norm_g[l].reshape(1, -1),
                             _pack_w_uq_t(w_uq[l]), wuk, wuvt, tables, seq)
        oa = mla_attn(qt, k, vt, batch, seq)
        obs, lses = zip(*[dil_attn(proj, batch, seq, rate) for rate in DIL_RATES])
        xf = out_proj(oa, obs, lses, out_norm_a_g[l].reshape(1, -1), out_norm_b_g[l].reshape(1, -1),
                      w_out[l].astype(BF16), xf, g1, seq)

        h2, info, counts = router(xf, norm2_g[l].reshape(1, d), sc2, sh2, rw_hi, rw_lo, rb, tri, seq)
        d1, d2, row_tok, blk_e, n_used = _dispatch(info, counts, t)
        xs = jnp.take(h2, row_tok, axis=0, mode="clip")
        wgu = jnp.concatenate([exp_gate[l], exp_up[l]], axis=-1).astype(BF16)
        y = moe_ffn(blk_e, n_used, xs, wgu, exp_down[l].astype(BF16))
        y1 = jnp.take(y, d1, axis=0, mode="clip")
        y2 = jnp.take(y, d2, axis=0, mode="clip")
        w12 = jnp.stack([info[4], info[5]], axis=1)
        xf = combine(xf, y1, y2, w12, g2, final_g.reshape(1, d), seq, final=(l == depth - 1))
    return xf.reshape(batch, seq, d)
```
